```python
import math
import jax, jax.numpy as jnp
from jax import lax
import numpy as np

D_MODEL = 1024
BATCH = 8
SEQ = 4096
DEPTH = 1

N_META = 16
NORM_EPS = 1e-6
SSM_WIDTH = D_MODEL // 2
SSM_GROUP = 16
SSM_GROUPS = SSM_WIDTH // SSM_GROUP
SSM_STATE = 64
ML_WIDTH = D_MODEL
ML_HEADS = 4
ML_HEAD_DIM = ML_WIDTH // ML_HEADS
ML_CHUNK = 64
ML_CONV = 4
ML_QKV_BLOCK = 4
PEER_HEADS = 8
PEER_NKEYS = 128
PEER_EXPERTS = PEER_NKEYS * PEER_NKEYS
PEER_TOPK = 16
PEER_QDIM = 256
PEER_HALF = PEER_QDIM // 2
PEER_TOKEN_BLOCK = 256
IN_COLS = SSM_WIDTH + 2 * ML_WIDTH + 2 * D_MODEL

kernel_name = "hybrid_s5_mlstm_peer_block"


def rmsnorm(x, g):
    x32 = x.astype(jnp.float32)
    y = x32 * lax.rsqrt(jnp.mean(x32 * x32, axis=-1, keepdims=True) + NORM_EPS)
    return (y * g.astype(jnp.float32)).astype(x.dtype)


def s5_branch(u, a_re, a_im, log_dt, b_re, b_im, c_re, c_im, d, w_glu, b_glu):
    f32 = jnp.float32
    bn, L, _ = u.shape
    u32 = u.astype(f32).reshape(bn, L, SSM_GROUPS, SSM_GROUP)
    lam = lax.complex(a_re.astype(f32), a_im.astype(f32))
    dt = jnp.exp(log_dt.astype(f32))[:, None]
    a_bar = jnp.exp(lam * dt)
    b_bar = ((a_bar - 1.0) / lam)[..., None] * lax.complex(b_re.astype(f32), b_im.astype(f32))
    bu = jnp.einsum('blgc,gpc->blgp', u32.astype(jnp.complex64), b_bar)
    a_seq = jnp.broadcast_to(a_bar, (1, L) + a_bar.shape)

    def combine(e1, e2):
        a1, b1 = e1
        a2, b2 = e2
        return a1 * a2, a2 * b1 + b2

    _, states = lax.associative_scan(combine, (a_seq, bu), axis=1)
    c = lax.complex(c_re.astype(f32), c_im.astype(f32))
    y = jnp.real(jnp.einsum('blgp,gcp->blgc', states, c)) + d.astype(f32).reshape(SSM_GROUPS, SSM_GROUP) * u32
    y = jax.nn.gelu(y.reshape(bn, L, SSM_WIDTH))
    y = y * jax.nn.sigmoid(y @ w_glu.astype(f32) + b_glu.astype(f32))
    return y.astype(u.dtype)


def causal_depthwise_conv(x, w, b):
    K = w.shape[0]
    L = x.shape[1]
    xp = jnp.pad(x, ((0, 0), (K - 1, 0), (0, 0)))
    return sum(xp[:, j:j + L] * w[j] for j in range(K)) + b


def blockdiag(x, w):
    sh = x.shape
    xb = x.reshape(sh[:-1] + (w.shape[0], w.shape[1]))
    return jnp.einsum('blnc,ncd->blnd', xb, w).reshape(sh)


def mlstm_chunk_step(carry, xs):
    c_mat, n_vec, m_st = carry
    q, k, v, log_i, log_f = xs
    b = jnp.cumsum(log_f, axis=-1)
    g = b[..., -1]
    causal = jnp.tril(jnp.ones((ML_CHUNK, ML_CHUNK), dtype=bool))
    d_log = jnp.where(causal, b[..., :, None] - b[..., None, :] + log_i[..., None, :], -jnp.inf)
    inter_log = b + m_st[..., None]
    m_t = jnp.maximum(inter_log, jnp.max(d_log, axis=-1))
    w_intra = jnp.exp(d_log - m_t[..., None])
    s = jnp.einsum('bhtd,bhsd->bhts', q, k) * w_intra
    w_inter = jnp.exp(inter_log - m_t)
    num = jnp.einsum('bhts,bhsd->bhtd', s, v) + w_inter[..., None] * jnp.einsum('bhed,bhtd->bhte', c_mat, q)
    den = jnp.sum(s, axis=-1) + w_inter * jnp.einsum('bhd,bhtd->bht', n_vec, q)
    h = num / jnp.maximum(jnp.abs(den), jnp.exp(-m_t))[..., None]
    a = g[..., None] - b + log_i
    m_new = jnp.maximum(g + m_st, jnp.max(a, axis=-1))
    w_state = jnp.exp(a - m_new[..., None])
    decay = jnp.exp(g + m_st - m_new)
    c_new = decay[..., None, None] * c_mat + jnp.einsum('bhs,bhse,bhsd->bhed', w_state, v, k)
    n_new = decay[..., None] * n_vec + jnp.einsum('bhs,bhsd->bhd', w_state, k)
    return (c_new, n_new, m_new), h


def mlstm_branch(xm, z, conv_w, conv_b, wq, wk, wv, w_if, b_if, norm_g, skip):
    f32 = jnp.float32
    bn, L, _ = xm.shape
    H, dh = ML_HEADS, ML_HEAD_DIM
    xc = jax.nn.silu(causal_depthwise_conv(xm, conv_w, conv_b))
    q = blockdiag(xc, wq)
    k = blockdiag(xc, wk)
    v = blockdiag(xm, wv)
    gates = (jnp.concatenate([q, k, v], axis=-1) @ w_if + b_if).astype(f32)
    log_i = gates[..., :H]
    log_f = jax.nn.log_sigmoid(gates[..., H:])
    k = k * (dh ** -0.5)
    pad_f = (-N_META) % ML_CHUNK
    pad_b = (-(L + pad_f)) % ML_CHUNK
    n_chunks = (L + pad_f + pad_b) // ML_CHUNK

    def heads_to_chunks(t):
        t = jnp.pad(t.astype(f32).reshape(bn, L, H, dh), ((0, 0), (pad_f, pad_b), (0, 0), (0, 0)))
        return t.reshape(bn, n_chunks, ML_CHUNK, H, dh).transpose(1, 0, 3, 2, 4)

    def gate_to_chunks(t):
        t = jnp.pad(t, ((0, 0), (pad_f, pad_b), (0, 0)))
        return t.reshape(bn, n_chunks, ML_CHUNK, H).transpose(1, 0, 3, 2)

    init = (jnp.zeros((bn, H, dh, dh), f32), jnp.zeros((bn, H, dh), f32), jnp.zeros((bn, H), f32))
    _, hc = lax.scan(mlstm_chunk_step, init,
                     (heads_to_chunks(q), heads_to_chunks(k), heads_to_chunks(v),
                      gate_to_chunks(log_i), gate_to_chunks(log_f)))
    h = hc.transpose(1, 0, 3, 2, 4).reshape(bn, n_chunks * ML_CHUNK, H, dh)[:, pad_f:pad_f + L]
    h = jax.nn.sigmoid(z.astype(f32)).reshape(bn, L, H, dh) * h
    h = h * lax.rsqrt(jnp.mean(h * h, axis=-1, keepdims=True) + NORM_EPS)
    h = h.reshape(bn, L, ML_WIDTH) * norm_g.astype(f32) + skip.astype(f32) * xc.astype(f32)
    return h.astype(xm.dtype)


def peer_ffn(xn, w_q, sub_keys, u_tab, v_tab):
    f32 = jnp.float32
    bn, L, D = xn.shape
    T = bn * L
    n_blk = -(-T // PEER_TOKEN_BLOCK)
    xt = jnp.pad(xn.reshape(T, D), ((0, n_blk * PEER_TOKEN_BLOCK - T), (0, 0)))
    xt = xt.reshape(n_blk, PEER_TOKEN_BLOCK, D)
    keys32 = sub_keys.astype(f32)

    def block(xb):
        q = (xb @ w_q).astype(f32).reshape(PEER_TOKEN_BLOCK, PEER_HEADS, 2, PEER_HALF)
        s = jnp.einsum('thid,hikd->thik', q, keys32)
        sv, si = lax.top_k(s, PEER_TOPK)
        cand = sv[..., 0, :, None] + sv[..., 1, None, :]
        cidx = si[..., 0, :, None] * PEER_NKEYS + si[..., 1, None, :]
        cand = cand.reshape(PEER_TOKEN_BLOCK, PEER_HEADS, PEER_TOPK * PEER_TOPK)
        cidx = cidx.reshape(PEER_TOKEN_BLOCK, PEER_HEADS, PEER_TOPK * PEER_TOPK)
        cs, ci = lax.top_k(cand, PEER_TOPK)
        eidx = jnp.take_along_axis(cidx, ci, axis=-1)
        gate = jax.nn.softmax(cs, axis=-1)
        ug = u_tab[eidx]
        act = jax.nn.gelu(jnp.einsum('td,thkd->thk', xb, ug).astype(f32))
        vg = v_tab[eidx]
        return jnp.einsum('thk,thkd->td', (gate * act).astype(vg.dtype), vg)

    out = lax.map(block, xt).reshape(n_blk * PEER_TOKEN_BLOCK, D)[:T]
    return out.reshape(bn, L, D).astype(xn.dtype)


def setup_inputs(seed: int = 0) -> dict:
    key = jax.random.key(seed)
    ks = list(jax.random.split(key, 40))
    f32 = jnp.float32
    it = iter(ks)

    def nrm(shape, scale):
        return jax.random.normal(next(it), shape, f32) * scale

    def gain(shape):
        return 1.0 + nrm(shape, 0.02)

    Dp = DEPTH
    x = nrm((BATCH, SEQ, D_MODEL), 1.0)
    meta_tokens = nrm((N_META, D_MODEL), 1.0)
    norm1_g = gain((Dp, D_MODEL))
    w_in = nrm((Dp, D_MODEL, IN_COLS), D_MODEL ** -0.5)
    ssm_a_re = -0.5 * jnp.exp(nrm((Dp, SSM_GROUPS, SSM_STATE), 0.05))
    ssm_a_im = jnp.pi * jnp.arange(SSM_STATE, dtype=f32) + nrm((Dp, SSM_GROUPS, SSM_STATE), 0.01)
    ssm_log_dt = jax.random.uniform(next(it), (Dp, SSM_GROUPS), f32, math.log(1e-3), math.log(1e-1))
    ssm_b_re = nrm((Dp, SSM_GROUPS, SSM_STATE, SSM_GROUP), (2 * SSM_GROUP) ** -0.5)
    ssm_b_im = nrm((Dp, SSM_GROUPS, SSM_STATE, SSM_GROUP), (2 * SSM_GROUP) ** -0.5)
    ssm_c_re = nrm((Dp, SSM_GROUPS, SSM_GROUP, SSM_STATE), (2 * SSM_STATE) ** -0.5)
    ssm_c_im = nrm((Dp, SSM_GROUPS, SSM_GROUP, SSM_STATE), (2 * SSM_STATE) ** -0.5)
    ssm_d = nrm((Dp, SSM_WIDTH), 0.5)
    ssm_w_glu = nrm((Dp, SSM_WIDTH, SSM_WIDTH), SSM_WIDTH ** -0.5)
    ssm_b_glu = nrm((Dp, SSM_WIDTH), 0.02)
    w_ssm_out = nrm((Dp, SSM_WIDTH, D_MODEL), SSM_WIDTH ** -0.5)
    ml_conv_w = nrm((Dp, ML_CONV, ML_WIDTH), 0.5)
    ml_conv_b = nrm((Dp, ML_WIDTH), 0.02)
    nb = ML_WIDTH // ML_QKV_BLOCK
    ml_wq = nrm((Dp, nb, ML_QKV_BLOCK, ML_QKV_BLOCK), ML_QKV_BLOCK ** -0.5)
    ml_wk = nrm((Dp, nb, ML_QKV_BLOCK, ML_QKV_BLOCK), ML_QKV_BLOCK ** -0.5)
    ml_wv = nrm((Dp, nb, ML_QKV_BLOCK, ML_QKV_BLOCK), ML_QKV_BLOCK ** -0.5)
    ml_w_if = nrm((Dp, 3 * ML_WIDTH, 2 * ML_HEADS), (3 * ML_WIDTH) ** -0.5)
    ml_b_if = jnp.concatenate([nrm((Dp, ML_HEADS), 0.1),
                               3.0 + 3.0 * jax.random.uniform(next(it), (Dp, ML_HEADS), f32)], axis=-1)
    ml_norm_g = gain((Dp, ML_WIDTH))
    ml_skip = 1.0 + nrm((Dp, ML_WIDTH), 0.1)
    w_ml_out = nrm((Dp, ML_WIDTH, D_MODEL), ML_WIDTH ** -0.5)
    w_out = nrm((Dp, D_MODEL, D_MODEL), D_MODEL ** -0.5)
    norm2_g = gain((Dp, D_MODEL))
    peer_w_q = nrm((Dp, D_MODEL, PEER_HEADS * PEER_QDIM), D_MODEL ** -0.5)
    peer_sub_keys = nrm((Dp, PEER_HEADS, 2, PEER_NKEYS, PEER_HALF), PEER_HALF ** -0.5)
    peer_u = nrm((Dp, PEER_EXPERTS, D_MODEL), D_MODEL ** -0.5)
    peer_v = nrm((Dp, PEER_EXPERTS, D_MODEL), 0.5 * PEER_HEADS ** -0.5)
    final_norm_g = gain((D_MODEL,))
    return {"x": x, "meta_tokens": meta_tokens, "norm1_g": norm1_g, "w_in": w_in,
            "ssm_a_re": ssm_a_re, "ssm_a_im": ssm_a_im, "ssm_log_dt": ssm_log_dt,
            "ssm_b_re": ssm_b_re, "ssm_b_im": ssm_b_im, "ssm_c_re": ssm_c_re, "ssm_c_im": ssm_c_im,
            "ssm_d": ssm_d, "ssm_w_glu": ssm_w_glu, "ssm_b_glu": ssm_b_glu, "w_ssm_out": w_ssm_out,
            "ml_conv_w": ml_conv_w, "ml_conv_b": ml_conv_b, "ml_wq": ml_wq, "ml_wk": ml_wk, "ml_wv": ml_wv,
            "ml_w_if": ml_w_if, "ml_b_if": ml_b_if, "ml_norm_g": ml_norm_g, "ml_skip": ml_skip,
            "w_ml_out": w_ml_out, "w_out": w_out, "norm2_g": norm2_g, "peer_w_q": peer_w_q,
            "peer_sub_keys": peer_sub_keys, "peer_u": peer_u, "peer_v": peer_v,
            "final_norm_g": final_norm_g}


def reference(x, meta_tokens, norm1_g, w_in, ssm_a_re, ssm_a_im, ssm_log_dt, ssm_b_re, ssm_b_im,
              ssm_c_re, ssm_c_im, ssm_d, ssm_w_glu, ssm_b_glu, w_ssm_out, ml_conv_w, ml_conv_b,
              ml_wq, ml_wk, ml_wv, ml_w_if, ml_b_if, ml_norm_g, ml_skip, w_ml_out, w_out, norm2_g,
              peer_w_q, peer_sub_keys, peer_u, peer_v, final_norm_g):
    bn = x.shape[0]
    meta = jnp.broadcast_to(meta_tokens.astype(x.dtype)[None], (bn, N_META, D_MODEL))
    h = jnp.concatenate([meta, x], axis=1)
    splits = [SSM_WIDTH, SSM_WIDTH + ML_WIDTH, SSM_WIDTH + 2 * ML_WIDTH,
              SSM_WIDTH + 2 * ML_WIDTH + D_MODEL]
    for l in range(DEPTH):
        xn = rmsnorm(h, norm1_g[l])
        proj = xn @ w_in[l]
        u_s, x_m, z_m, g_s, g_m = jnp.split(proj, splits, axis=-1)
        y_s = s5_branch(u_s, ssm_a_re[l], ssm_a_im[l], ssm_log_dt[l], ssm_b_re[l], ssm_b_im[l],
                        ssm_c_re[l], ssm_c_im[l], ssm_d[l], ssm_w_glu[l], ssm_b_glu[l]) @ w_ssm_out[l]
        y_m = mlstm_branch(x_m, z_m, ml_conv_w[l], ml_conv_b[l], ml_wq[l], ml_wk[l], ml_wv[l],
                           ml_w_if[l], ml_b_if[l], ml_norm_g[l], ml_skip[l]) @ w_ml_out[l]
        mixed = jax.nn.sigmoid(g_s) * y_s + jax.nn.sigmoid(g_m) * y_m
        h = h + mixed @ w_out[l]
        h = h + peer_ffn(rmsnorm(h, norm2_g[l]), peer_w_q[l], peer_sub_keys[l], peer_u[l], peer_v[l])
    return rmsnorm(h, final_norm_g)[:, N_META:]
```

```python
import functools
import math

import jax
import jax.numpy as jnp
from jax import lax
from jax.experimental import pallas as pl
from jax.experimental.pallas import tpu as pltpu

F32 = jnp.float32
BF16 = jnp.bfloat16
HIGHEST = lax.Precision.HIGHEST

N_META = 16
NORM_EPS = 1e-6
SSM_GROUP = 16
SSM_STATE = 64
ML_HEADS = 4
ML_CHUNK = 64
PAD_FRONT = (-N_META) % ML_CHUNK
PEER_HEADS = 8
PEER_NKEYS = 128
PEER_TOPK = 16
LANES = 128
ROW_BLOCK = 512
S5_TIME_BLOCK = 64
PEER_EXPERT_BLOCK = 1024
VMEM_LIMIT = 56 * 1024 * 1024

_NT = (((1,), (1,)), ((), ()))
_TN = (((0,), (0,)), ((), ()))


def _dot(a, b):
    return jnp.dot(a, b, preferred_element_type=F32)


def _sigmoid(x):
    return 1.0 / (1.0 + jnp.exp(-x))


def _rms(x, g):
    return x * lax.rsqrt(jnp.mean(x * x, axis=-1, keepdims=True) + NORM_EPS) * g


def _inproj_kernel(h_ref, g1_ref, wu_ref, wx_ref, wz_ref, wgs_ref, wgm_ref, cw_ref, cb_ref,
                   wqk_ref, wv_ref, wifq_ref, wifk_ref, wifv_ref, bif_ref,
                   us_ref, sz_ref, sgs_ref, sgm_ref, xc_ref, q_ref, k_ref, v_ref, li_ref, lf_ref,
                   xm_s):
    rows = h_ref.shape[0]
    d = h_ref.shape[1]

    @pl.when(pl.program_id(0) == 0)
    def _():
        xm_s[0:8, :] = jnp.zeros((8, d), F32)

    xn = _rms(h_ref[...], g1_ref[...]).astype(BF16)
    us_ref[...] = _dot(xn, wu_ref[...]).astype(BF16)
    sz_ref[...] = _sigmoid(_dot(xn, wz_ref[...])).astype(BF16)
    sgs_ref[...] = _sigmoid(_dot(xn, wgs_ref[...])).astype(BF16)
    sgm_ref[...] = _sigmoid(_dot(xn, wgm_ref[...])).astype(BF16)
    xm = _dot(xn, wx_ref[...])

    xm_s[8:8 + rows, :] = xm
    cw = cw_ref[...]
    conv = (xm * cw[3:4, :] + xm_s[7:7 + rows, :] * cw[2:3, :] + xm_s[6:6 + rows, :] * cw[1:2, :]
            + xm_s[5:5 + rows, :] * cw[0:1, :] + cb_ref[...])
    xm_s[0:8, :] = xm_s[rows:rows + 8, :]
    xc = (conv * _sigmoid(conv)).astype(BF16)
    xc_ref[...] = xc
    xmb = xm.astype(BF16)

    nt = wv_ref.shape[0]
    tw = wv_ref.shape[1]
    for t in range(nt):
        sl = slice(tw * t, tw * (t + 1))
        qk = _dot(xc[:, sl], wqk_ref[t])
        q_ref[:, sl] = qk[:, :tw].astype(BF16)
        k_ref[:, sl] = qk[:, tw:].astype(BF16)
        v_ref[:, sl] = _dot(xmb[:, sl], wv_ref[t]).astype(BF16)

    gates = (_dot(q_ref[...], wifq_ref[...]) + _dot(k_ref[...], wifk_ref[...])
             + _dot(v_ref[...], wifv_ref[...]) + bif_ref[...])
    li_ref[...] = gates[:, :LANES]
    gf = gates[:, LANES:]
    lf_ref[...] = jnp.minimum(gf, 0.0) - jnp.log(1.0 + jnp.exp(-jnp.abs(gf)))


def _inproj(h0, norm1_g, w_in, conv_w, conv_b, wq, wk, wv, w_if, b_if, ssm_w):
    t_rows, d = h0.shape
    ml_w = d
    rows = ROW_BLOCK
    assert t_rows % rows == 0
    c0, c1, c2, c3 = ssm_w, ssm_w + ml_w, ssm_w + 2 * ml_w, ssm_w + 2 * ml_w + d
    w_in = w_in.astype(BF16)
    wu, wx, wz, wgs, wgm = w_in[:, :c0], w_in[:, c0:c1], w_in[:, c1:c2], w_in[:, c2:c3], w_in[:, c3:]

    tile = 2 * LANES
    nblk = wq.shape[0]
    bs = wq.shape[1]
    per_tile = tile // bs
    ntile = nblk // per_tile

    def expand(w):
        w = w.reshape(ntile, per_tile, bs, bs)
        eye = jnp.eye(per_tile, dtype=w.dtype)
        full = eye[None, :, None, :, None] * w[:, :, :, None, :]
        return full.reshape(ntile, tile, tile)

    wqk = jnp.concatenate([expand(wq), expand(wk)], axis=-1).astype(BF16)
    wvt = expand(wv).astype(BF16)

    nh = w_if.shape[1] // 2

    def gate_cols(w):
        z = jnp.zeros((w.shape[0], LANES - nh), w.dtype)
        return jnp.concatenate([w[:, :nh], z, w[:, nh:], z], axis=1).astype(BF16)

    wifq, wifk, wifv = gate_cols(w_if[:ml_w]), gate_cols(w_if[ml_w:2 * ml_w]), gate_cols(w_if[2 * ml_w:])
    zb = jnp.zeros((LANES - nh,), F32)
    bif = jnp.concatenate([b_if[:nh], zb, b_if[nh:], zb])[None, :]

    row_spec = lambda c: pl.BlockSpec((rows, c), lambda i: (i, 0))
    full = lambda a: pl.BlockSpec(a.shape, lambda i: (0,) * a.ndim)
    args = (h0, norm1_g[None, :], wu, wx, wz, wgs, wgm, conv_w, conv_b[None, :],
            wqk, wvt, wifq, wifk, wifv, bif)
    out_shapes = ([jax.ShapeDtypeStruct((t_rows, ssm_w), BF16)]
                  + [jax.ShapeDtypeStruct((t_rows, d), BF16)] * 7
                  + [jax.ShapeDtypeStruct((t_rows, LANES), F32)] * 2)
    return pl.pallas_call(
        _inproj_kernel,
        grid=(t_rows // rows,),
        in_specs=[row_spec(d)] + [full(a) for a in args[1:]],
        out_specs=[row_spec(s.shape[1]) for s in out_shapes],
        out_shape=out_shapes,
        scratch_shapes=[pltpu.VMEM((rows + 8, d), F32)],
        compiler_params=pltpu.CompilerParams(dimension_semantics=("arbitrary",),
                                             vmem_limit_bytes=VMEM_LIMIT),
        name="inproj",
    )(*args)


def _s5_kernel(u_ref, bm_ref, cm_ref, ar_ref, ai_ref, d_ref, wglu_ref, bglu_ref, wso_ref,
               y_ref, st_ref, x_ref):
    nb, tc, w = u_ref.shape
    nslab = st_ref.shape[0]
    half = nslab // 2
    group = 4

    @pl.when(pl.program_id(0) == 0)
    def _():
        x_ref[...] = jnp.zeros(x_ref.shape, F32)

    u = u_ref[...].reshape(nb * tc, w)
    bu = _dot(u, bm_ref[...])
    for j in range(nslab):
        st_ref[j] = bu[:, j * LANES:(j + 1) * LANES]

    for g0 in range(0, half, group):
        js = list(range(g0, g0 + group))
        ar = [jnp.broadcast_to(ar_ref[:, j * LANES:(j + 1) * LANES], (nb, LANES)) for j in js]
        ai = [jnp.broadcast_to(ai_ref[:, j * LANES:(j + 1) * LANES], (nb, LANES)) for j in js]
        xr0 = tuple(x_ref[j] for j in js)
        xi0 = tuple(x_ref[half + j] for j in js)

        def body(s, carry, js=js, ar=ar, ai=ai):
            xr, xi = carry
            nr, ni = [], []
            for q, j in enumerate(js):
                rows = pl.ds(s, nb, stride=tc)
                r = ar[q] * xr[q] - ai[q] * xi[q] + st_ref[j, rows, :]
                i = ar[q] * xi[q] + ai[q] * xr[q] + st_ref[half + j, rows, :]
                st_ref[j, rows, :] = r
                st_ref[half + j, rows, :] = i
                nr.append(r)
                ni.append(i)
            return tuple(nr), tuple(ni)

        xr, xi = lax.fori_loop(0, tc, body, (xr0, xi0), unroll=2)
        for q, j in enumerate(js):
            x_ref[j] = xr[q]
            x_ref[half + j] = xi[q]

    states = jnp.concatenate([st_ref[j].astype(BF16) for j in range(nslab)], axis=-1)
    y = _dot(states, cm_ref[...]) + d_ref[...] * u.astype(F32)
    y = jax.nn.gelu(y, approximate=True)
    y = y * _sigmoid(_dot(y.astype(BF16), wglu_ref[...]) + bglu_ref[...])
    out = _dot(y.astype(BF16), wso_ref[...])
    y_ref[...] = out.reshape(nb, tc, out.shape[-1]).astype(BF16)


def _s5(us, a_re, a_im, log_dt, b_re, b_im, c_re, c_im, d_skip, w_glu, b_glu, w_so):
    nb, lp, w = us.shape
    ng, ns = a_re.shape
    gs = w // ng
    d_out = w_so.shape[1]
    tc = S5_TIME_BLOCK
    assert lp % tc == 0

    lam = lax.complex(a_re, a_im)
    dt = jnp.exp(log_dt)[:, None]
    a_bar = jnp.exp(lam * dt)
    b_bar = ((a_bar - 1.0) / lam)[..., None] * lax.complex(b_re, b_im)
    eye = jnp.eye(ng, dtype=F32)

    def blockdiag(m):
        return (eye[:, None, :, None] * m[:, :, None, :]).reshape(ng * m.shape[1], ng * m.shape[2])

    b_t = jnp.transpose(b_bar, (0, 2, 1))
    bm = jnp.concatenate([blockdiag(jnp.real(b_t)), blockdiag(jnp.imag(b_t))], axis=1).astype(BF16)
    c_re_t = jnp.transpose(c_re, (0, 2, 1))
    c_im_t = jnp.transpose(c_im, (0, 2, 1))
    cm = jnp.concatenate([blockdiag(c_re_t), blockdiag(-c_im_t)], axis=0).astype(BF16)
    ar = jnp.real(a_bar).reshape(1, ng * ns)
    ai = jnp.imag(a_bar).reshape(1, ng * ns)
    nslab = 2 * ng * ns // LANES

    full = lambda a: pl.BlockSpec(a.shape, lambda t: (0,) * a.ndim)
    args = (us, bm, cm, ar, ai, d_skip[None, :], w_glu.astype(BF16), b_glu[None, :], w_so.astype(BF16))
    return pl.pallas_call(
        _s5_kernel,
        grid=(lp // tc,),
        in_specs=[pl.BlockSpec((nb, tc, w), lambda t: (0, t, 0))] + [full(a) for a in args[1:]],
        out_specs=pl.BlockSpec((nb, tc, d_out), lambda t: (0, t, 0)),
        out_shape=jax.ShapeDtypeStruct((nb, lp, d_out), BF16),
        scratch_shapes=[pltpu.VMEM((nslab, nb * tc, LANES), F32),
                        pltpu.VMEM((nslab, nb, LANES), F32)],
        compiler_params=pltpu.CompilerParams(dimension_semantics=("arbitrary",),
                                             vmem_limit_bytes=VMEM_LIMIT),
        name="s5",
    )(*args)


def _mlstm_kernel(q_ref, k_ref, v_ref, xc_ref, sz_ref, sgs_ref, sgm_ref, ys_ref, h0_ref, li_ref, lf_ref,
                  ng_ref, skip_ref, wmo_ref, wout_ref, lt_ref, eye_ref,
                  h1_ref, c_ref, n_ref, m_ref, ym_ref):
    nb, lc, d = q_ref.shape
    nh = ML_HEADS
    dh = d // nh
    c = pl.program_id(0)

    @pl.when(c == 0)
    def _():
        c_ref[...] = jnp.zeros(c_ref.shape, F32)
        n_ref[...] = jnp.zeros(n_ref.shape, F32)
        m_ref[...] = jnp.zeros(m_ref.shape, F32)

    row = lax.broadcasted_iota(jnp.int32, (lc, 1), 0)
    valid = jnp.logical_or(c > 0, row >= PAD_FRONT)
    validf = jnp.where(valid, 1.0, 0.0).astype(F32)
    validb = validf.astype(BF16)
    kscale = (validf * (dh ** -0.5)).astype(BF16)
    tri = lax.broadcasted_iota(jnp.int32, (lc, lc), 0) >= lax.broadcasted_iota(jnp.int32, (lc, lc), 1)

    def per_batch(b, carry):
        li = li_ref[b] * validf
        lf = lf_ref[b] * validf
        bc = jnp.dot(lt_ref[...], lf, precision=HIGHEST, preferred_element_type=F32)
        rows_t = lax.dot_general(eye_ref[...], li - bc, _NT, precision=HIGHEST,
                                 preferred_element_type=F32)
        qb = q_ref[b] * validb
        kb = k_ref[b] * kscale
        vb = v_ref[b] * validb
        szb = sz_ref[b]
        xcb = xc_ref[b]
        for h in range(nh):
            sl = slice(dh * h, dh * (h + 1))
            idx = b * nh + h
            q, k, v = qb[:, sl], kb[:, sl], vb[:, sl]
            b_col = bc[:, h:h + 1]
            li_col = li[:, h:h + 1]
            g = bc[lc - 1:lc, h:h + 1]
            m_prev = m_ref[idx][:, 0:1]
            n_prev = n_ref[idx]
            c_prev = c_ref[idx]

            d_log = jnp.where(tri, b_col + rows_t[h:h + 1, :], -jnp.inf)
            inter_log = b_col + m_prev
            m_t = jnp.maximum(inter_log, jnp.max(d_log, axis=-1, keepdims=True))
            w_intra = jnp.exp(d_log - m_t)
            s = lax.dot_general(q, k, _NT, preferred_element_type=F32) * w_intra
            w_inter = jnp.exp(inter_log - m_t)
            qc = lax.dot_general(q, c_prev.astype(BF16), _NT, preferred_element_type=F32)
            num = _dot(s.astype(BF16), v) + w_inter * qc
            qn = jnp.sum(q.astype(F32) * n_prev, axis=-1, keepdims=True)
            den = jnp.sum(s, axis=-1, keepdims=True) + w_inter * qn
            hh = num / jnp.maximum(jnp.abs(den), jnp.exp(-m_t))

            a = g - b_col + li_col
            m_new = jnp.maximum(g + m_prev, jnp.max(a, axis=0, keepdims=True))
            w_state = jnp.exp(a - m_new)
            decay = jnp.exp(g + m_prev - m_new)
            kf = k.astype(F32)
            wv = (w_state * v.astype(F32)).astype(BF16)
            c_ref[idx] = decay * c_prev + lax.dot_general(wv, k, _TN, preferred_element_type=F32)
            n_ref[idx] = decay * n_prev + jnp.sum(w_state * kf, axis=0, keepdims=True)
            m_ref[idx] = jnp.broadcast_to(m_new, m_ref.shape[1:])

            o = szb[:, sl].astype(F32) * hh
            o = o * lax.rsqrt(jnp.mean(o * o, axis=-1, keepdims=True) + NORM_EPS)
            ym = o * ng_ref[:, sl] + skip_ref[:, sl] * xcb[:, sl].astype(F32)
            ym_ref[pl.ds(pl.multiple_of(b * lc, lc), lc), sl] = ym.astype(BF16)
        return carry

    lax.fori_loop(0, nb, per_batch, 0)

    rows = nb * lc
    y_m = _dot(ym_ref[...], wmo_ref[...])
    mixed = (sgs_ref[...].reshape(rows, d).astype(F32) * ys_ref[...].reshape(rows, d).astype(F32)
             + sgm_ref[...].reshape(rows, d).astype(F32) * y_m)
    h1 = h0_ref[...].reshape(rows, d) + _dot(mixed.astype(BF16), wout_ref[...])
    h1_ref[...] = h1.reshape(nb, lc, d)


def _mlstm(q, k, v, xc, sz, sgs, sgm, ys, h0, li, lf, norm_g, skip, w_mo, w_out):
    nb, lp, d = q.shape
    lc = ML_CHUNK
    nh = ML_HEADS
    dh = d // nh
    lt = jnp.tril(jnp.ones((lc, lc), F32))
    eye = jnp.eye(8, LANES, dtype=F32)
    blk = lambda w: pl.BlockSpec((nb, lc, w), lambda c: (0, c, 0))
    full = lambda a: pl.BlockSpec(a.shape, lambda c: (0,) * a.ndim)
    consts = (norm_g[None, :], skip[None, :], w_mo.astype(BF16), w_out.astype(BF16), lt, eye)
    return pl.pallas_call(
        _mlstm_kernel,
        grid=(lp // lc,),
        in_specs=[blk(d)] * 9 + [blk(LANES)] * 2 + [full(a) for a in consts],
        out_specs=blk(d),
        out_shape=jax.ShapeDtypeStruct((nb, lp, d), F32),
        scratch_shapes=[pltpu.VMEM((nb * nh, dh, dh), F32),
                        pltpu.VMEM((nb * nh, 1, dh), F32),
                        pltpu.VMEM((nb * nh, 1, LANES), F32),
                        pltpu.VMEM((nb * lc, d), BF16)],
        compiler_params=pltpu.CompilerParams(dimension_semantics=("arbitrary",),
                                             vmem_limit_bytes=VMEM_LIMIT),
        name="mlstm",
    )(q, k, v, xc, sz, sgs, sgm, ys, h0, li, lf, *consts)


def _candidate_pairs():
    return [(p, q) for p in range(PEER_TOPK) for q in range(PEER_TOPK) if (p + 1) * (q + 1) <= PEER_TOPK]


def _peer_kernel(h1_ref, g2_ref, gf_ref, wq_ref, keys_ref, u_ref, vt_ref, out_ref,
                 xnt_ref, s_ref, e_ref, tau_ref, top_ref, xu_ref, w_ref, acc_ref):
    tb, d = h1_ref.shape
    nlg = tb // LANES
    nhx = keys_ref.shape[0]
    nk = PEER_NKEYS
    topk = PEER_TOPK
    eb = u_ref.shape[0]
    per_step = eb // nk
    j = pl.program_id(1)
    neg = -jnp.inf

    @pl.when(j == 0)
    def _prologue():
        xn = _rms(h1_ref[...], g2_ref[...])
        xnt_ref[...] = xn.T.astype(BF16)
        qq = _dot(xn.astype(BF16), wq_ref[...])
        for hx in range(nhx):
            st = lax.dot_general(keys_ref[hx], qq[:, hx * nk:(hx + 1) * nk].astype(BF16), _NT,
                                 preferred_element_type=F32)
            for lg in range(nlg):
                s_ref[lg, hx] = st[:, lg * LANES:(lg + 1) * LANES]

        def per_lane_group(lg, carry):
            for hx in range(nhx):
                x = s_ref[lg, hx]
                for p in range(topk):
                    m = jnp.max(x, axis=0, keepdims=True)
                    top_ref[lg, hx * topk + p:hx * topk + p + 1, :] = m
                    if p + 1 < topk:
                        x = jnp.where(x == m, neg, x)
            a = [top_ref[lg, pl.ds(p, PEER_HEADS, stride=2 * topk), :] for p in range(topk)]
            bq = [top_ref[lg, pl.ds(topk + p, PEER_HEADS, stride=2 * topk), :] for p in range(topk)]
            cand = [a[p] + bq[q] for (p, q) in _candidate_pairs()]
            cmax = a[0] + bq[0]
            z = jnp.zeros_like(cmax)
            m = cmax
            for r in range(topk):
                m = functools.reduce(jnp.maximum, cand)
                z = z + jnp.exp(m - cmax)
                if r + 1 < topk:
                    cand = [jnp.where(cv == m, neg, cv) for cv in cand]
            tau_ref[lg] = m
            zinv = 1.0 / z
            for h in range(PEER_HEADS):
                e_ref[lg, 2 * h] = jnp.exp(s_ref[lg, 2 * h] - a[0][h:h + 1, :]) * zinv[h:h + 1, :]
                e_ref[lg, 2 * h + 1] = jnp.exp(s_ref[lg, 2 * h + 1] - bq[0][h:h + 1, :])
            return carry

        lax.fori_loop(0, nlg, per_lane_group, 0)
        acc_ref[...] = jnp.zeros(acc_ref.shape, F32)

    xu_ref[...] = _dot(u_ref[...], xnt_ref[...])

    def per_first_key(ii, carry):
        i = j * per_step + ii
        r0 = pl.multiple_of(ii * nk, nk)
        for lg in range(nlg):
            lanes = slice(lg * LANES, (lg + 1) * LANES)
            gate = jnp.zeros((nk, LANES), F32)
            for h in range(PEER_HEADS):
                s1 = s_ref[lg, 2 * h, pl.ds(i, 1), :]
                e1 = e_ref[lg, 2 * h, pl.ds(i, 1), :]
                sel = (s1 + s_ref[lg, 2 * h + 1]) >= tau_ref[lg, h:h + 1, :]
                gate = gate + e1 * jnp.where(sel, e_ref[lg, 2 * h + 1], 0.0)
            act = jax.nn.gelu(xu_ref[pl.ds(r0, nk), lanes], approximate=True)
            w_ref[pl.ds(r0, nk), lanes] = (act * gate).astype(BF16)
        return carry

    lax.fori_loop(0, per_step, per_first_key, 0)
    acc_ref[...] += _dot(vt_ref[...], w_ref[...])

    @pl.when(j == pl.num_programs(1) - 1)
    def _epilogue():
        h2 = h1_ref[...] + acc_ref[...].T
        out_ref[...] = _rms(h2, gf_ref[...])


def _peer(h1, norm2_g, final_g, w_q, sub_keys, u_tab, v_tab):
    t_rows, d = h1.shape
    tb = ROW_BLOCK
    assert t_rows % tb == 0
    nlg = tb // LANES
    nheads, _, nk, half = sub_keys.shape
    nhx = nheads * 2
    ne = u_tab.shape[0]
    eb = PEER_EXPERT_BLOCK
    keys = sub_keys.reshape(nhx, nk, half).astype(BF16)
    u_b = u_tab.astype(BF16)
    vt_b = v_tab.astype(BF16).T
    wq_b = w_q.astype(BF16)
    return pl.pallas_call(
        _peer_kernel,
        grid=(t_rows // tb, ne // eb),
        in_specs=[pl.BlockSpec((tb, d), lambda i, j: (i, 0)),
                  pl.BlockSpec((1, d), lambda i, j: (0, 0)),
                  pl.BlockSpec((1, d), lambda i, j: (0, 0)),
                  pl.BlockSpec(wq_b.shape, lambda i, j: (0, 0)),
                  pl.BlockSpec(keys.shape, lambda i, j: (0, 0, 0)),
                  pl.BlockSpec((eb, d), lambda i, j: (j, 0)),
                  pl.BlockSpec((d, eb), lambda i, j: (0, j))],
        out_specs=pl.BlockSpec((tb, d), lambda i, j: (i, 0)),
        out_shape=jax.ShapeDtypeStruct((t_rows, d), F32),
        scratch_shapes=[pltpu.VMEM((d, tb), BF16),
                        pltpu.VMEM((nlg, nhx, nk, LANES), F32),
                        pltpu.VMEM((nlg, nhx, nk, LANES), F32),
                        pltpu.VMEM((nlg, nheads, LANES), F32),
                        pltpu.VMEM((nlg, nhx * PEER_TOPK, LANES), F32),
                        pltpu.VMEM((eb, tb), F32),
                        pltpu.VMEM((eb, tb), BF16),
                        pltpu.VMEM((d, tb), F32)],
        compiler_params=pltpu.CompilerParams(dimension_semantics=("parallel", "arbitrary"),
                                             vmem_limit_bytes=VMEM_LIMIT),
        name="peer",
    )(h1, norm2_g[None, :], final_g[None, :], wq_b, keys, u_b, vt_b)


def kernel(x, meta_tokens, norm1_g, w_in, ssm_a_re, ssm_a_im, ssm_log_dt, ssm_b_re, ssm_b_im, ssm_c_re,
           ssm_c_im, ssm_d, ssm_w_glu, ssm_b_glu, w_ssm_out, ml_conv_w, ml_conv_b, ml_wq, ml_wk, ml_wv,
           ml_w_if, ml_b_if, ml_norm_g, ml_skip, w_ml_out, w_out, norm2_g, peer_w_q, peer_sub_keys,
           peer_u, peer_v, final_norm_g):
    nb, seq, d = x.shape
    depth = w_in.shape[0]
    assert depth == 1, "the final norm is fused into the last layer's PEER kernel"
    lp = PAD_FRONT + N_META + seq
    assert lp % ML_CHUNK == 0
    ssm_w = ssm_d.shape[1]
    meta = jnp.broadcast_to(meta_tokens.astype(x.dtype)[None], (nb, N_META, d))
    h = jnp.concatenate([jnp.zeros((nb, PAD_FRONT, d), x.dtype), meta, x], axis=1)
    for l in range(depth):
        hf = h.reshape(nb * lp, d)
        us, sz, sgs, sgm, xc, q, k, v, li, lf = _inproj(
            hf, norm1_g[l], w_in[l], ml_conv_w[l], ml_conv_b[l], ml_wq[l], ml_wk[l], ml_wv[l],
            ml_w_if[l], ml_b_if[l], ssm_w)
        r3 = lambda a: a.reshape(nb, lp, a.shape[-1])
        ys = _s5(r3(us), ssm_a_re[l], ssm_a_im[l], ssm_log_dt[l], ssm_b_re[l], ssm_b_im[l], ssm_c_re[l],
                 ssm_c_im[l], ssm_d[l], ssm_w_glu[l], ssm_b_glu[l], w_ssm_out[l])
        h1 = _mlstm(r3(q), r3(k), r3(v), r3(xc), r3(sz), r3(sgs), r3(sgm), ys, h, r3(li), r3(lf),
                    ml_norm_g[l], ml_skip[l], w_ml_out[l], w_out[l])
        h = _peer(h1.reshape(nb * lp, d), norm2_g[l], final_norm_g, peer_w_q[l], peer_sub_keys[l],
                  peer_u[l], peer_v[l]).reshape(nb, lp, d)
    return h[:, PAD_FRONT + N_META:]
```

```python
import functools
import math

import jax
import jax.numpy as jnp
from jax import lax
from jax.experimental import pallas as pl
from jax.experimental.pallas import tpu as pltpu

F32 = jnp.float32
BF16 = jnp.bfloat16
HIGHEST = lax.Precision.HIGHEST

N_META = 16
NORM_EPS = 1e-6
SSM_GROUP = 16
SSM_STATE = 64
ML_HEADS = 4
ML_CHUNK = 64
PAD_FRONT = (-N_META) % ML_CHUNK
PEER_HEADS = 8
PEER_NKEYS = 128
PEER_TOPK = 16
LANES = 128
ROW_BLOCK = 512
S5_TIME_BLOCK = 64
PEER_EXPERT_BLOCK = 1024
VMEM_LIMIT = 56 * 1024 * 1024

_NT = (((1,), (1,)), ((), ()))
_TN = (((0,), (0,)), ((), ()))


def _dot(a, b):
    return jnp.dot(a, b, preferred_element_type=F32)


def _sigmoid(x):
    return 1.0 / (1.0 + jnp.exp(-x))


def _rms(x, g):
    return x * lax.rsqrt(jnp.mean(x * x, axis=-1, keepdims=True) + NORM_EPS) * g


def _inproj_kernel(h_ref, g1_ref, wu_ref, wx_ref, wz_ref, wgs_ref, wgm_ref, cw_ref, cb_ref,
                   wqk_ref, wv_ref, wifq_ref, wifk_ref, wifv_ref, bif_ref,
                   us_ref, sz_ref, sgs_ref, sgm_ref, xc_ref, q_ref, k_ref, v_ref, li_ref, lf_ref,
                   xm_s):
    rows = h_ref.shape[0]
    d = h_ref.shape[1]

    @pl.when(pl.program_id(0) == 0)
    def _():
        xm_s[0:8, :] = jnp.zeros((8, d), F32)

    xn = _rms(h_ref[...], g1_ref[...]).astype(BF16)
    us_ref[...] = _dot(xn, wu_ref[...]).astype(BF16)
    sz_ref[...] = _sigmoid(_dot(xn, wz_ref[...])).astype(BF16)
    sgs_ref[...] = _sigmoid(_dot(xn, wgs_ref[...])).astype(BF16)
    sgm_ref[...] = _sigmoid(_dot(xn, wgm_ref[...])).astype(BF16)
    xm = _dot(xn, wx_ref[...])

    xm_s[8:8 + rows, :] = xm
    cw = cw_ref[...]
    conv = (xm * cw[3:4, :] + xm_s[7:7 + rows, :] * cw[2:3, :] + xm_s[6:6 + rows, :] * cw[1:2, :]
            + xm_s[5:5 + rows, :] * cw[0:1, :] + cb_ref[...])
    xm_s[0:8, :] = xm_s[rows:rows + 8, :]
    xc = (conv * _sigmoid(conv)).astype(BF16)
    xc_ref[...] = xc
    xmb = xm.astype(BF16)

    nt = wv_ref.shape[0]
    tw = wv_ref.shape[1]
    for t in range(nt):
        sl = slice(tw * t, tw * (t + 1))
        qk = _dot(xc[:, sl], wqk_ref[t])
        q_ref[:, sl] = qk[:, :tw].astype(BF16)
        k_ref[:, sl] = qk[:, tw:].astype(BF16)
        v_ref[:, sl] = _dot(xmb[:, sl], wv_ref[t]).astype(BF16)

    gates = (_dot(q_ref[...], wifq_ref[...]) + _dot(k_ref[...], wifk_ref[...])
             + _dot(v_ref[...], wifv_ref[...]) + bif_ref[...])
    li_ref[...] = gates[:, :LANES]
    gf = gates[:, LANES:]
    lf_ref[...] = jnp.minimum(gf, 0.0) - jnp.log(1.0 + jnp.exp(-jnp.abs(gf)))


def _inproj(h0, norm1_g, w_in, conv_w, conv_b, wq, wk, wv, w_if, b_if, ssm_w):
    t_rows, d = h0.shape
    ml_w = d
    rows = ROW_BLOCK
    assert t_rows % rows == 0
    c0, c1, c2, c3 = ssm_w, ssm_w + ml_w, ssm_w + 2 * ml_w, ssm_w + 2 * ml_w + d
    w_in = w_in.astype(BF16)
    wu, wx, wz, wgs, wgm = w_in[:, :c0], w_in[:, c0:c1], w_in[:, c1:c2], w_in[:, c2:c3], w_in[:, c3:]

    tile = 2 * LANES
    nblk = wq.shape[0]
    bs = wq.shape[1]
    per_tile = tile // bs
    ntile = nblk // per_tile

    def expand(w):
        w = w.reshape(ntile, per_tile, bs, bs)
        eye = jnp.eye(per_tile, dtype=w.dtype)
        full = eye[None, :, None, :, None] * w[:, :, :, None, :]
        return full.reshape(ntile, tile, tile)

    wqk = jnp.concatenate([expand(wq), expand(wk)], axis=-1).astype(BF16)
    wvt = expand(wv).astype(BF16)

    nh = w_if.shape[1] // 2

    def gate_cols(w):
        z = jnp.zeros((w.shape[0], LANES - nh), w.dtype)
        return jnp.concatenate([w[:, :nh], z, w[:, nh:], z], axis=1).astype(BF16)

    wifq, wifk, wifv = gate_cols(w_if[:ml_w]), gate_cols(w_if[ml_w:2 * ml_w]), gate_cols(w_if[2 * ml_w:])
    zb = jnp.zeros((LANES - nh,), F32)
    bif = jnp.concatenate([b_if[:nh], zb, b_if[nh:], zb])[None, :]

    row_spec = lambda c: pl.BlockSpec((rows, c), lambda i: (i, 0))
    full = lambda a: pl.BlockSpec(a.shape, lambda i: (0,) * a.ndim)
    args = (h0, norm1_g[None, :], wu, wx, wz, wgs, wgm, conv_w, conv_b[None, :],
            wqk, wvt, wifq, wifk, wifv, bif)
    out_shapes = ([jax.ShapeDtypeStruct((t_rows, ssm_w), BF16)]
                  + [jax.ShapeDtypeStruct((t_rows, d), BF16)] * 7
                  + [jax.ShapeDtypeStruct((t_rows, LANES), F32)] * 2)
    return pl.pallas_call(
        _inproj_kernel,
        grid=(t_rows // rows,),
        in_specs=[row_spec(d)] + [full(a) for a in args[1:]],
        out_specs=[row_spec(s.shape[1]) for s in out_shapes],
        out_shape=out_shapes,
        scratch_shapes=[pltpu.VMEM((rows + 8, d), F32)],
        compiler_params=pltpu.CompilerParams(dimension_semantics=("arbitrary",),
                                             vmem_limit_bytes=VMEM_LIMIT),
        name="inproj",
    )(*args)


def _s5_kernel(u_ref, bm_ref, cm_ref, ar_ref, ai_ref, d_ref, wglu_ref, bglu_ref, wso_ref,
               y_ref, st_ref, x_ref):
    nb, tc, w = u_ref.shape
    nslab = st_ref.shape[0]
    half = nslab // 2
    group = 4

    @pl.when(pl.program_id(0) == 0)
    def _():
        x_ref[...] = jnp.zeros(x_ref.shape, F32)

    u = u_ref[...].reshape(nb * tc, w)
    bu = _dot(u, bm_ref[...])
    for j in range(nslab):
        st_ref[j] = bu[:, j * LANES:(j + 1) * LANES]

    for g0 in range(0, half, group):
        js = list(range(g0, g0 + group))
        ar = [jnp.broadcast_to(ar_ref[:, j * LANES:(j + 1) * LANES], (nb, LANES)) for j in js]
        ai = [jnp.broadcast_to(ai_ref[:, j * LANES:(j + 1) * LANES], (nb, LANES)) for j in js]
        xr0 = tuple(x_ref[j] for j in js)
        xi0 = tuple(x_ref[half + j] for j in js)

        def body(s, carry, js=js, ar=ar, ai=ai):
            xr, xi = carry
            nr, ni = [], []
            for q, j in enumerate(js):
                rows = pl.ds(s, nb, stride=tc)
                r = ar[q] * xr[q] - ai[q] * xi[q] + st_ref[j, rows, :]
                i = ar[q] * xi[q] + ai[q] * xr[q] + st_ref[half + j, rows, :]
                st_ref[j, rows, :] = r
                st_ref[half + j, rows, :] = i
                nr.append(r)
                ni.append(i)
            return tuple(nr), tuple(ni)

        xr, xi = lax.fori_loop(0, tc, body, (xr0, xi0), unroll=2)
        for q, j in enumerate(js):
            x_ref[j] = xr[q]
            x_ref[half + j] = xi[q]

    states = jnp.concatenate([st_ref[j].astype(BF16) for j in range(nslab)], axis=-1)
    y = _dot(states, cm_ref[...]) + d_ref[...] * u.astype(F32)
    y = jax.nn.gelu(y, approximate=True)
    y = y * _sigmoid(_dot(y.astype(BF16), wglu_ref[...]) + bglu_ref[...])
    out = _dot(y.astype(BF16), wso_ref[...])
    y_ref[...] = out.reshape(nb, tc, out.shape[-1]).astype(BF16)


def _s5(us, a_re, a_im, log_dt, b_re, b_im, c_re, c_im, d_skip, w_glu, b_glu, w_so):
    nb, lp, w = us.shape
    ng, ns = a_re.shape
    gs = w // ng
    d_out = w_so.shape[1]
    tc = S5_TIME_BLOCK
    assert lp % tc == 0

    lam = lax.complex(a_re, a_im)
    dt = jnp.exp(log_dt)[:, None]
    a_bar = jnp.exp(lam * dt)
    b_bar = ((a_bar - 1.0) / lam)[..., None] * lax.complex(b_re, b_im)
    eye = jnp.eye(ng, dtype=F32)

    def blockdiag(m):
        return (eye[:, None, :, None] * m[:, :, None, :]).reshape(ng * m.shape[1], ng * m.shape[2])

    b_t = jnp.transpose(b_bar, (0, 2, 1))
    bm = jnp.concatenate([blockdiag(jnp.real(b_t)), blockdiag(jnp.imag(b_t))], axis=1).astype(BF16)
    c_re_t = jnp.transpose(c_re, (0, 2, 1))
    c_im_t = jnp.transpose(c_im, (0, 2, 1))
    cm = jnp.concatenate([blockdiag(c_re_t), blockdiag(-c_im_t)], axis=0).astype(BF16)
    ar = jnp.real(a_bar).reshape(1, ng * ns)
    ai = jnp.imag(a_bar).reshape(1, ng * ns)
    nslab = 2 * ng * ns // LANES

    full = lambda a: pl.BlockSpec(a.shape, lambda t: (0,) * a.ndim)
    args = (us, bm, cm, ar, ai, d_skip[None, :], w_glu.astype(BF16), b_glu[None, :], w_so.astype(BF16))
    return pl.pallas_call(
        _s5_kernel,
        grid=(lp // tc,),
        in_specs=[pl.BlockSpec((nb, tc, w), lambda t: (0, t, 0))] + [full(a) for a in args[1:]],
        out_specs=pl.BlockSpec((nb, tc, d_out), lambda t: (0, t, 0)),
        out_shape=jax.ShapeDtypeStruct((nb, lp, d_out), BF16),
        scratch_shapes=[pltpu.VMEM((nslab, nb * tc, LANES), F32),
                        pltpu.VMEM((nslab, nb, LANES), F32)],
        compiler_params=pltpu.CompilerParams(dimension_semantics=("arbitrary",),
                                             vmem_limit_bytes=VMEM_LIMIT),
        name="s5",
    )(*args)


def _mlstm_kernel(q_ref, k_ref, v_ref, xc_ref, sz_ref, sgs_ref, sgm_ref, ys_ref, h0_ref, li_ref, lf_ref,
                  ng_ref, skip_ref, wmo_ref, wout_ref, lt_ref, eye_ref,
                  h1_ref, c_ref, n_ref, m_ref, ym_ref):
    nb, lc, d = q_ref.shape
    nh = ML_HEADS
    dh = d // nh
    c = pl.program_id(0)

    @pl.when(c == 0)
    def _():
        c_ref[...] = jnp.zeros(c_ref.shape, F32)
        n_ref[...] = jnp.zeros(n_ref.shape, F32)
        m_ref[...] = jnp.zeros(m_ref.shape, F32)

    row = lax.broadcasted_iota(jnp.int32, (lc, 1), 0)
    valid = jnp.logical_or(c > 0, row >= PAD_FRONT)
    validf = jnp.where(valid, 1.0, 0.0).astype(F32)
    validb = validf.astype(BF16)
    kscale = (validf * (dh ** -0.5)).astype(BF16)
    tri = lax.broadcasted_iota(jnp.int32, (lc, lc), 0) >= lax.broadcasted_iota(jnp.int32, (lc, lc), 1)

    def per_batch(b, carry):
        li = li_ref[b] * validf
        lf = lf_ref[b] * validf
        bc = jnp.dot(lt_ref[...], lf, precision=HIGHEST, preferred_element_type=F32)
        rows_t = lax.dot_general(eye_ref[...], li - bc, _NT, precision=HIGHEST,
                                 preferred_element_type=F32)
        qb = q_ref[b] * validb
        kb = k_ref[b] * kscale
        vb = v_ref[b] * validb
        szb = sz_ref[b]
        xcb = xc_ref[b]
        for h in range(nh):
            sl = slice(dh * h, dh * (h + 1))
            idx = b * nh + h
            q, k, v = qb[:, sl], kb[:, sl], vb[:, sl]
            b_col = bc[:, h:h + 1]
            li_col = li[:, h:h + 1]
            g = bc[lc - 1:lc, h:h + 1]
            m_prev = m_ref[idx][:, 0:1]
            n_prev = n_ref[idx]
            c_prev = c_ref[idx]

            d_log = jnp.where(tri, b_col + rows_t[h:h + 1, :], -jnp.inf)
            inter_log = b_col + m_prev
            m_t = jnp.maximum(inter_log, jnp.max(d_log, axis=-1, keepdims=True))
            w_intra = jnp.exp(d_log - m_t)
            s = lax.dot_general(q, k, _NT, preferred_element_type=F32) * w_intra
            w_inter = jnp.exp(inter_log - m_t)
            qc = lax.dot_general(q, c_prev.astype(BF16), _NT, preferred_element_type=F32)
            num = _dot(s.astype(BF16), v) + w_inter * qc
            qn = jnp.sum(q.astype(F32) * n_prev, axis=-1, keepdims=True)
            den = jnp.sum(s, axis=-1, keepdims=True) + w_inter * qn
            hh = num / jnp.maximum(jnp.abs(den), jnp.exp(-m_t))

            a = g - b_col + li_col
            m_new = jnp.maximum(g + m_prev, jnp.max(a, axis=0, keepdims=True))
            w_state = jnp.exp(a - m_new)
            decay = jnp.exp(g + m_prev - m_new)
            kf = k.astype(F32)
            wv = (w_state * v.astype(F32)).astype(BF16)
            c_ref[idx] = decay * c_prev + lax.dot_general(wv, k, _TN, preferred_element_type=F32)
            n_ref[idx] = decay * n_prev + jnp.sum(w_state * kf, axis=0, keepdims=True)
            m_ref[idx] = jnp.broadcast_to(m_new, m_ref.shape[1:])

            o = szb[:, sl].astype(F32) * hh
            o = o * lax.rsqrt(jnp.mean(o * o, axis=-1, keepdims=True) + NORM_EPS)
            ym = o * ng_ref[:, sl] + skip_ref[:, sl] * xcb[:, sl].astype(F32)
            ym_ref[pl.ds(pl.multiple_of(b * lc, lc), lc), sl] = ym.astype(BF16)
        return carry

    lax.fori_loop(0, nb, per_batch, 0)

    rows = nb * lc
    y_m = _dot(ym_ref[...], wmo_ref[...])
    mixed = (sgs_ref[...].reshape(rows, d).astype(F32) * ys_ref[...].reshape(rows, d).astype(F32)
             + sgm_ref[...].reshape(rows, d).astype(F32) * y_m)
    h1 = h0_ref[...].reshape(rows, d) + _dot(mixed.astype(BF16), wout_ref[...])
    h1_ref[...] = h1.reshape(nb, lc, d)


def _mlstm(q, k, v, xc, sz, sgs, sgm, ys, h0, li, lf, norm_g, skip, w_mo, w_out):
    nb, lp, d = q.shape
    lc = ML_CHUNK
    nh = ML_HEADS
    dh = d // nh
    lt = jnp.tril(jnp.ones((lc, lc), F32))
    eye = jnp.eye(8, LANES, dtype=F32)
    blk = lambda w: pl.BlockSpec((nb, lc, w), lambda c: (0, c, 0))
    full = lambda a: pl.BlockSpec(a.shape, lambda c: (0,) * a.ndim)
    consts = (norm_g[None, :], skip[None, :], w_mo.astype(BF16), w_out.astype(BF16), lt, eye)
    return pl.pallas_call(
        _mlstm_kernel,
        grid=(lp // lc,),
        in_specs=[blk(d)] * 9 + [blk(LANES)] * 2 + [full(a) for a in consts],
        out_specs=blk(d),
        out_shape=jax.ShapeDtypeStruct((nb, lp, d), F32),
        scratch_shapes=[pltpu.VMEM((nb * nh, dh, dh), F32),
                        pltpu.VMEM((nb * nh, 1, dh), F32),
                        pltpu.VMEM((nb * nh, 1, LANES), F32),
                        pltpu.VMEM((nb * lc, d), BF16)],
        compiler_params=pltpu.CompilerParams(dimension_semantics=("arbitrary",),
                                             vmem_limit_bytes=VMEM_LIMIT),
        name="mlstm",
    )(q, k, v, xc, sz, sgs, sgm, ys, h0, li, lf, *consts)


def _candidate_pairs():
    return [(p, q) for p in range(PEER_TOPK) for q in range(PEER_TOPK) if (p + 1) * (q + 1) <= PEER_TOPK]


def _peer_kernel(h1_ref, g2_ref, gf_ref, wq_ref, keys_ref, u_ref, vt_ref, out_ref,
                 xnt_ref, s_ref, e_ref, tau_ref, top_ref, xua_ref, xub_ref, w_ref, acc_ref):
    tb, d = h1_ref.shape
    nlg = tb // LANES
    nhx = keys_ref.shape[0]
    nk = PEER_NKEYS
    topk = PEER_TOPK
    eb = u_ref.shape[0]
    per_step = eb // nk
    j = pl.program_id(1)
    neg = -jnp.inf

    @pl.when(j == 0)
    def _prologue():
        xn = _rms(h1_ref[...], g2_ref[...])
        xnt_ref[...] = xn.T.astype(BF16)
        qq = _dot(xn.astype(BF16), wq_ref[...])
        for hx in range(nhx):
            st = lax.dot_general(keys_ref[hx], qq[:, hx * nk:(hx + 1) * nk].astype(BF16), _NT,
                                 preferred_element_type=F32)
            for lg in range(nlg):
                s_ref[lg, hx] = st[:, lg * LANES:(lg + 1) * LANES]

        def per_lane_group(lg, carry):
            for hx in range(nhx):
                x = s_ref[lg, hx]
                for p in range(topk):
                    m = jnp.max(x, axis=0, keepdims=True)
                    top_ref[lg, hx * topk + p:hx * topk + p + 1, :] = m
                    if p + 1 < topk:
                        x = jnp.where(x == m, neg, x)
            a = [top_ref[lg, pl.ds(p, PEER_HEADS, stride=2 * topk), :] for p in range(topk)]
            bq = [top_ref[lg, pl.ds(topk + p, PEER_HEADS, stride=2 * topk), :] for p in range(topk)]
            cand = [a[p] + bq[q] for (p, q) in _candidate_pairs()]
            cmax = a[0] + bq[0]
            z = jnp.zeros_like(cmax)
            m = cmax
            for r in range(topk):
                m = functools.reduce(jnp.maximum, cand)
                z = z + jnp.exp(m - cmax)
                if r + 1 < topk:
                    cand = [jnp.where(cv == m, neg, cv) for cv in cand]
            tau_ref[lg] = m
            zinv = 1.0 / z
            for h in range(PEER_HEADS):
                e_ref[lg, 2 * h] = jnp.exp(s_ref[lg, 2 * h] - a[0][h:h + 1, :]) * zinv[h:h + 1, :]
                e_ref[lg, 2 * h + 1] = jnp.exp(s_ref[lg, 2 * h + 1] - bq[0][h:h + 1, :])
            return carry

        lax.fori_loop(0, nlg, per_lane_group, 0)
        acc_ref[...] = jnp.zeros(acc_ref.shape, F32)

    def preact(dst_ref):
        dst_ref[...] = _dot(u_ref[...], xnt_ref[...])

    def gate_and_project(blk, src_ref):
        for ii in range(per_step):
            i = blk * per_step + ii
            rows = slice(ii * nk, (ii + 1) * nk)
            for lg in range(nlg):
                lanes = slice(lg * LANES, (lg + 1) * LANES)
                gate = jnp.zeros((nk, LANES), F32)
                for h in range(PEER_HEADS):
                    s1 = s_ref[lg, 2 * h, pl.ds(i, 1), :]
                    e1 = e_ref[lg, 2 * h, pl.ds(i, 1), :]
                    sel = (s1 + s_ref[lg, 2 * h + 1]) >= tau_ref[lg, h:h + 1, :]
                    gate = gate + e1 * jnp.where(sel, e_ref[lg, 2 * h + 1], 0.0)
                act = jax.nn.gelu(src_ref[rows, lanes], approximate=True)
                w_ref[rows, lanes] = (act * gate).astype(BF16)
        acc_ref[...] += _dot(vt_ref[...], w_ref[...])

    nblk = pl.num_programs(1) - 1
    odd = j % 2 == 1

    @pl.when(j == 0)
    def _first():
        preact(xua_ref)

    @pl.when(jnp.logical_and(odd, j < nblk))
    def _odd():
        preact(xub_ref)
        gate_and_project(j - 1, xua_ref)

    @pl.when(jnp.logical_and(jnp.logical_not(odd), jnp.logical_and(j > 0, j < nblk)))
    def _even():
        preact(xua_ref)
        gate_and_project(j - 1, xub_ref)

    @pl.when(j == nblk)
    def _last():
        gate_and_project(j - 1, xub_ref)
        h2 = h1_ref[...] + acc_ref[...].T
        out_ref[...] = _rms(h2, gf_ref[...])


def _peer(h1, norm2_g, final_g, w_q, sub_keys, u_tab, v_tab):
    t_rows, d = h1.shape
    tb = ROW_BLOCK
    assert t_rows % tb == 0
    nlg = tb // LANES
    nheads, _, nk, half = sub_keys.shape
    nhx = nheads * 2
    ne = u_tab.shape[0]
    eb = PEER_EXPERT_BLOCK
    nblk = ne // eb
    assert nblk % 2 == 0, "the last expert block must land in the second pre-activation buffer"
    keys = sub_keys.reshape(nhx, nk, half).astype(BF16)
    u_b = u_tab.astype(BF16)
    vt_b = v_tab.astype(BF16).T
    wq_b = w_q.astype(BF16)
    return pl.pallas_call(
        _peer_kernel,
        grid=(t_rows // tb, nblk + 1),
        in_specs=[pl.BlockSpec((tb, d), lambda i, j: (i, 0)),
                  pl.BlockSpec((1, d), lambda i, j: (0, 0)),
                  pl.BlockSpec((1, d), lambda i, j: (0, 0)),
                  pl.BlockSpec(wq_b.shape, lambda i, j: (0, 0)),
                  pl.BlockSpec(keys.shape, lambda i, j: (0, 0, 0)),
                  pl.BlockSpec((eb, d), lambda i, j: (jnp.minimum(j, nblk - 1), 0)),
                  pl.BlockSpec((d, eb), lambda i, j: (0, jnp.maximum(j - 1, 0)))],
        out_specs=pl.BlockSpec((tb, d), lambda i, j: (i, 0)),
        out_shape=jax.ShapeDtypeStruct((t_rows, d), F32),
        scratch_shapes=[pltpu.VMEM((d, tb), BF16),
                        pltpu.VMEM((nlg, nhx, nk, LANES), F32),
                        pltpu.VMEM((nlg, nhx, nk, LANES), F32),
                        pltpu.VMEM((nlg, nheads, LANES), F32),
                        pltpu.VMEM((nlg, nhx * PEER_TOPK, LANES), F32),
                        pltpu.VMEM((eb, tb), F32),
                        pltpu.VMEM((eb, tb), F32),
                        pltpu.VMEM((eb, tb), BF16),
                        pltpu.VMEM((d, tb), F32)],
        compiler_params=pltpu.CompilerParams(dimension_semantics=("parallel", "arbitrary"),
                                             vmem_limit_bytes=VMEM_LIMIT),
        name="peer",
    )(h1, norm2_g[None, :], final_g[None, :], wq_b, keys, u_b, vt_b)


def kernel(x, meta_tokens, norm1_g, w_in, ssm_a_re, ssm_a_im, ssm_log_dt, ssm_b_re, ssm_b_im, ssm_c_re,
           ssm_c_im, ssm_d, ssm_w_glu, ssm_b_glu, w_ssm_out, ml_conv_w, ml_conv_b, ml_wq, ml_wk, ml_wv,
           ml_w_if, ml_b_if, ml_norm_g, ml_skip, w_ml_out, w_out, norm2_g, peer_w_q, peer_sub_keys,
           peer_u, peer_v, final_norm_g):
    nb, seq, d = x.shape
    depth = w_in.shape[0]
    assert depth == 1, "the final norm is fused into the last layer's PEER kernel"
    lp = PAD_FRONT + N_META + seq
    assert lp % ML_CHUNK == 0
    ssm_w = ssm_d.shape[1]
    meta = jnp.broadcast_to(meta_tokens.astype(x.dtype)[None], (nb, N_META, d))
    h = jnp.concatenate([jnp.zeros((nb, PAD_FRONT, d), x.dtype), meta, x], axis=1)
    for l in range(depth):
        hf = h.reshape(nb * lp, d)
        us, sz, sgs, sgm, xc, q, k, v, li, lf = _inproj(
            hf, norm1_g[l], w_in[l], ml_conv_w[l], ml_conv_b[l], ml_wq[l], ml_wk[l], ml_wv[l],
            ml_w_if[l], ml_b_if[l], ssm_w)
        r3 = lambda a: a.reshape(nb, lp, a.shape[-1])
        ys = _s5(r3(us), ssm_a_re[l], ssm_a_im[l], ssm_log_dt[l], ssm_b_re[l], ssm_b_im[l], ssm_c_re[l],
                 ssm_c_im[l], ssm_d[l], ssm_w_glu[l], ssm_b_glu[l], w_ssm_out[l])
        h1 = _mlstm(r3(q), r3(k), r3(v), r3(xc), r3(sz), r3(sgs), r3(sgm), ys, h, r3(li), r3(lf),
                    ml_norm_g[l], ml_skip[l], w_ml_out[l], w_out[l])
        h = _peer(h1.reshape(nb * lp, d), norm2_g[l], final_norm_g, peer_w_q[l], peer_sub_keys[l],
                  peer_u[l], peer_v[l]).reshape(nb, lp, d)
    return h[:, PAD_FRONT + N_META:]
```
